```python
import math
import jax
import jax.numpy as jnp
from jax import lax
import numpy as np

D_MODEL = 2048
BATCH = 2
SEQ = 8192
DEPTH = 4

GRID_W = 64
CTX_LEN = 256
QBLK = 128
WINDOW = 128
ROPE_BASE = 10000.0
EPS = 1e-6
NEG_INF = -1e30

DIFF_HEADS = 8
DIFF_DK = 64
DIFF_DV = 128
MLA_HEADS = 8
MLA_NOPE = 128
MLA_ROPE = 64
MLA_DV = 128
MLA_Q_RANK = 512
MLA_KV_RANK = 256
SWA_HEADS = 8
SWA_KV_HEADS = 2
SWA_DH = 128
AX_HEADS = 8
AX_KV_HEADS = 2
AX_DH = 128
N_BRANCH = 4
BRANCH_W = 1024
N_EXPERTS = 32
TOP_K = 4
D_EXPERT = 768
SWIGLU_LIMIT = 7.0
SWIGLU_ALPHA = 1.702
EBLK = 128

IN_SPLITS = (
    DIFF_HEADS * 2 * DIFF_DK, DIFF_HEADS * 2 * DIFF_DK, DIFF_HEADS * DIFF_DV,
    MLA_Q_RANK, MLA_KV_RANK, MLA_ROPE,
    SWA_HEADS * SWA_DH, SWA_KV_HEADS * SWA_DH, SWA_KV_HEADS * SWA_DH,
    AX_HEADS * AX_DH, AX_KV_HEADS * AX_DH, AX_KV_HEADS * AX_DH,
    N_BRANCH * D_MODEL,
)
IN_COLS = sum(IN_SPLITS)

kernel_name = 'hybrid_gated_mixers_moe_dit'


def rms_norm(x, g):
    xf = x.astype(jnp.float32)
    y = xf * lax.rsqrt(jnp.mean(xf * xf, axis=-1, keepdims=True) + EPS)
    return (y * g.astype(jnp.float32)).astype(x.dtype)


def rope_tables(rows, cols, dim):
    quarter = dim // 4
    inv = ROPE_BASE ** (-jnp.arange(quarter, dtype=jnp.float32) / quarter)
    ang = jnp.concatenate([rows[:, None] * inv, cols[:, None] * inv], axis=-1)
    return jnp.cos(ang), jnp.sin(ang)


def apply_rope(x, cos, sin):
    half = x.shape[-1] // 2
    x1, x2 = x[..., :half], x[..., half:]
    c = cos[None, :, None, :].astype(x.dtype)
    s = sin[None, :, None, :].astype(x.dtype)
    return jnp.concatenate([x1 * c - x2 * s, x1 * s + x2 * c], axis=-1)


def split_columns(p):
    parts, start = [], 0
    for n in IN_SPLITS:
        parts.append(p[..., start:start + n])
        start += n
    return parts


def attend(q, k, v, scale, sink=None):
    s = jnp.einsum('bqgrd,bkgd->bgrqk', q, k).astype(jnp.float32) * scale
    if sink is not None:
        sk = jnp.broadcast_to(sink.astype(jnp.float32)[None, :, :, None, None], s.shape[:-1] + (1,))
        s = jnp.concatenate([s, sk], axis=-1)
    p = jax.nn.softmax(s, axis=-1)
    if sink is not None:
        p = p[..., :-1]
    return jnp.einsum('bgrqk,bkgd->bqgrd', p.astype(v.dtype), v)


def dense_latent_attention(q, k_all, v_all, scale):
    b, t, g, r, dk = q.shape
    nb = t // QBLK
    qb = jnp.moveaxis(q.reshape(b, nb, QBLK, g, r, dk), 1, 0)
    out = lax.map(lambda qi: attend(qi, k_all, v_all, scale), qb)
    return jnp.moveaxis(out, 0, 1).reshape(b, t, g, r, v_all.shape[-1])


def window_latent_attention(q, k, v, kc, vc, sink, scale):
    b, t, g, r, dk = q.shape
    dv = v.shape[-1]
    nb = t // QBLK
    pad = ((0, 0), (QBLK, QBLK), (0, 0), (0, 0))
    kp = jnp.pad(k, pad).reshape(b, nb + 2, QBLK, g, dk)
    vp = jnp.pad(v, pad).reshape(b, nb + 2, QBLK, g, dv)
    kw = jnp.concatenate([kp[:, :-2], kp[:, 1:-1], kp[:, 2:]], axis=2)
    vw = jnp.concatenate([vp[:, :-2], vp[:, 1:-1], vp[:, 2:]], axis=2)
    qb = q.reshape(b, nb, QBLK, g, r, dk)
    qpos = jnp.arange(t).reshape(nb, QBLK)
    kpos = (jnp.arange(nb)[:, None] - 1) * QBLK + jnp.arange(3 * QBLK)[None, :]
    ok = ((jnp.abs(kpos[:, None, :] - qpos[:, :, None]) <= WINDOW)
          & (kpos[:, None, :] >= 0) & (kpos[:, None, :] < t))
    bias = jnp.where(ok, 0.0, NEG_INF)[None, :, None, None]
    s_win = jnp.einsum('bnqgrd,bnkgd->bngrqk', qb, kw).astype(jnp.float32) * scale + bias
    s_ctx = jnp.einsum('bnqgrd,bkgd->bngrqk', qb, kc).astype(jnp.float32) * scale
    s_sink = jnp.broadcast_to(sink.astype(jnp.float32)[None, None, :, :, None, None], s_win.shape[:-1] + (1,))
    p = jax.nn.softmax(jnp.concatenate([s_win, s_ctx, s_sink], axis=-1), axis=-1)
    nw, nc = 3 * QBLK, kc.shape[1]
    o = (jnp.einsum('bngrqk,bnkgd->bnqgrd', p[..., :nw].astype(v.dtype), vw)
         + jnp.einsum('bngrqk,bkgd->bnqgrd', p[..., nw:nw + nc].astype(v.dtype), vc))
    return o.reshape(b, t, g, r, dv)


def diff_branch(pl, pc, qn_g, kn_g, lam, subln_g, lam_init, cos, sin, with_ctx):
    scale = DIFF_DK ** -0.5

    def heads(q, k, v, rotate):
        b, t, _ = q.shape
        q = rms_norm(q.reshape(b, t, 2 * DIFF_HEADS, DIFF_DK), qn_g)
        k = rms_norm(k.reshape(b, t, 2 * DIFF_HEADS, DIFF_DK), kn_g)
        if rotate:
            q, k = apply_rope(q, cos, sin), apply_rope(k, cos, sin)
        q = q.reshape(b, t, DIFF_HEADS, 2, DIFF_DK)
        k = k.reshape(b, t, DIFF_HEADS, 2, DIFF_DK)
        return (q[:, :, :, 0:1], q[:, :, :, 1:2], k[:, :, :, 0], k[:, :, :, 1],
                v.reshape(b, t, DIFF_HEADS, DIFF_DV))

    lf = lam.astype(jnp.float32)
    lam_val = jnp.exp(jnp.sum(lf[0] * lf[1])) - jnp.exp(jnp.sum(lf[2] * lf[3])) + lam_init

    def finish(o1, o2):
        b, t = o1.shape[:2]
        o = o1 - lam_val.astype(o1.dtype) * o2
        return (rms_norm(o, subln_g) * (1.0 - lam_init)).reshape(b, t, BRANCH_W)

    q1l, q2l, k1l, k2l, vl = heads(*pl, True)
    q1c, q2c, k1c, k2c, vc = heads(*pc, False)
    v_all = jnp.concatenate([vl, vc], axis=1)
    out_l = finish(dense_latent_attention(q1l, jnp.concatenate([k1l, k1c], axis=1), v_all, scale),
                   dense_latent_attention(q2l, jnp.concatenate([k2l, k2c], axis=1), v_all, scale))
    out_c = finish(attend(q1c, k1c, vc, scale), attend(q2c, k2c, vc, scale)) if with_ctx else None
    return out_l, out_c


def mla_branch(pl, pc, qa_g, w_uq, kva_g, w_ukv, qn_g, kn_g, cos, sin, with_ctx):
    scale = (MLA_NOPE + MLA_ROPE) ** -0.5

    def heads(cq, ckv, kr, rotate):
        b, t, _ = cq.shape
        q = (rms_norm(cq, qa_g) @ w_uq).reshape(b, t, MLA_HEADS, MLA_NOPE + MLA_ROPE)
        kv = (rms_norm(ckv, kva_g) @ w_ukv).reshape(b, t, MLA_HEADS, MLA_NOPE + MLA_DV)
        k_rope = jnp.broadcast_to(kr[:, :, None, :], (b, t, MLA_HEADS, MLA_ROPE))
        q = rms_norm(q, qn_g)
        k = rms_norm(jnp.concatenate([kv[..., :MLA_NOPE], k_rope], axis=-1), kn_g)
        if rotate:
            q = jnp.concatenate([q[..., :MLA_NOPE], apply_rope(q[..., MLA_NOPE:], cos, sin)], axis=-1)
            k = jnp.concatenate([k[..., :MLA_NOPE], apply_rope(k[..., MLA_NOPE:], cos, sin)], axis=-1)
        return q[:, :, :, None], k, kv[..., MLA_NOPE:]

    ql, kl, vl = heads(*pl, True)
    qc, kc, vc = heads(*pc, False)
    b, t = ql.shape[:2]
    out_l = dense_latent_attention(ql, jnp.concatenate([kl, kc], axis=1),
                                   jnp.concatenate([vl, vc], axis=1), scale).reshape(b, t, BRANCH_W)
    out_c = attend(qc, kc, vc, scale).reshape(qc.shape[0], qc.shape[1], BRANCH_W) if with_ctx else None
    return out_l, out_c


def gqa_heads(q, k, v, n_heads, n_kv, dh, qn_g, kn_g, cos, sin, rotate):
    b, t, _ = q.shape
    q = rms_norm(q.reshape(b, t, n_heads, dh), qn_g)
    k = rms_norm(k.reshape(b, t, n_kv, dh), kn_g)
    if rotate:
        q, k = apply_rope(q, cos, sin), apply_rope(k, cos, sin)
    return q.reshape(b, t, n_kv, n_heads // n_kv, dh), k, v.reshape(b, t, n_kv, dh)


def swa_branch(pl, pc, qn_g, kn_g, sink, cos, sin, with_ctx):
    scale = SWA_DH ** -0.5
    sink_gr = sink.reshape(SWA_KV_HEADS, SWA_HEADS // SWA_KV_HEADS)
    ql, kl, vl = gqa_heads(*pl, SWA_HEADS, SWA_KV_HEADS, SWA_DH, qn_g, kn_g, cos, sin, True)
    qc, kc, vc = gqa_heads(*pc, SWA_HEADS, SWA_KV_HEADS, SWA_DH, qn_g, kn_g, cos, sin, False)
    b, t = ql.shape[:2]
    out_l = window_latent_attention(ql, kl, vl, kc, vc, sink_gr, scale).reshape(b, t, BRANCH_W)
    out_c = attend(qc, kc, vc, scale, sink_gr).reshape(qc.shape[0], qc.shape[1], BRANCH_W) if with_ctx else None
    return out_l, out_c


def axial_branch(pl, pc, qn_g, kn_g, cos, sin, with_ctx):
    scale = AX_DH ** -0.5
    ql, kl, vl = gqa_heads(*pl, AX_HEADS, AX_KV_HEADS, AX_DH, qn_g, kn_g, cos, sin, True)
    qc, kc, vc = gqa_heads(*pc, AX_HEADS, AX_KV_HEADS, AX_DH, qn_g, kn_g, cos, sin, False)
    b, t = ql.shape[:2]
    out_l = dense_latent_attention(ql, jnp.concatenate([kl, kc], axis=1),
                                   jnp.concatenate([vl, vc], axis=1), scale).reshape(b, t, BRANCH_W)
    out_c = attend(qc, kc, vc, scale).reshape(qc.shape[0], qc.shape[1], BRANCH_W) if with_ctx else None
    return out_l, out_c


def merge_branches(outs, gate_logits, w_branch, w_out):
    b, t, _ = gate_logits.shape
    gates = jax.nn.sigmoid(gate_logits.astype(jnp.float32)).astype(gate_logits.dtype)
    gates = gates.reshape(b, t, N_BRANCH, D_MODEL)
    y = gates[:, :, 0] * (outs[0] @ w_branch[0])
    for i in range(1, N_BRANCH):
        y = y + gates[:, :, i] * (outs[i] @ w_branch[i])
    return y @ w_out


def moe_ffn(h, router_w, router_b, w_gu, b_gu, w_down, b_down):
    t, d = h.shape
    logits = (h @ router_w).astype(jnp.float32) + router_b.astype(jnp.float32)
    top_val, top_idx = lax.top_k(logits, TOP_K)
    top_w = jax.nn.softmax(top_val, axis=-1)
    n_assign = t * TOP_K
    flat_e = top_idx.reshape(-1)
    order = jnp.argsort(flat_e)
    sorted_e = flat_e[order]
    counts = jnp.bincount(flat_e, length=N_EXPERTS)
    padded = (counts + EBLK - 1) // EBLK * EBLK
    pad_end = jnp.cumsum(padded)
    pad_start = pad_end - padded
    start = jnp.cumsum(counts) - counts
    dest = pad_start[sorted_e] + jnp.arange(n_assign) - start[sorted_e]
    n_blocks = -(-n_assign // EBLK) + N_EXPERTS
    slot_tok = jnp.full((n_blocks * EBLK,), t, jnp.int32).at[dest].set((order // TOP_K).astype(jnp.int32))
    slot_w = jnp.zeros((n_blocks * EBLK,), h.dtype).at[dest].set(top_w.reshape(-1)[order].astype(h.dtype))
    blk_expert = jnp.minimum(jnp.searchsorted(pad_end, jnp.arange(n_blocks) * EBLK, side='right'), N_EXPERTS - 1)
    h_pad = jnp.concatenate([h, jnp.zeros((1, d), h.dtype)], axis=0)

    def run_block(args):
        tok, wt, e = args
        gu = h_pad[tok] @ w_gu[e] + b_gu[e]
        gate = jnp.minimum(gu[:, :D_EXPERT], SWIGLU_LIMIT)
        up = jnp.clip(gu[:, D_EXPERT:], -SWIGLU_LIMIT, SWIGLU_LIMIT)
        act = (up + 1.0) * gate * jax.nn.sigmoid(SWIGLU_ALPHA * gate)
        return (act @ w_down[e] + b_down[e]) * wt[:, None]

    yb = lax.map(run_block, (slot_tok.reshape(n_blocks, EBLK), slot_w.reshape(n_blocks, EBLK), blk_expert))
    return jax.ops.segment_sum(yb.reshape(-1, d), slot_tok, num_segments=t + 1)[:t]


def setup_inputs(seed: int = 0) -> dict:
    key = jax.random.key(seed)
    keys = jax.random.split(key, 40)
    it = iter([keys[i] for i in range(40)])

    def nrm(shape, scale):
        return jax.random.normal(next(it), shape, jnp.float32) * scale

    def gain(shape):
        return 1.0 + nrm(shape, 0.02)

    d = D_MODEL
    return {
        'x': nrm((BATCH, SEQ, d), 1.0),
        'c': nrm((BATCH, d), 1.0),
        'ctx': nrm((BATCH, CTX_LEN, d), 1.0),
        'c_ctx': nrm((d,), 1.0),
        'ada_w': nrm((DEPTH, d, 6 * d), 0.5 * d ** -0.5),
        'ada_b': nrm((DEPTH, 6 * d), 0.02),
        'norm1_g': gain((DEPTH, d)),
        'norm2_g': gain((DEPTH, d)),
        'w_in': nrm((DEPTH, d, IN_COLS), d ** -0.5),
        'diff_qn_g': gain((DEPTH, DIFF_DK)),
        'diff_kn_g': gain((DEPTH, DIFF_DK)),
        'diff_lambda': nrm((DEPTH, 4, DIFF_DK), 0.1),
        'diff_subln_g': gain((DEPTH, DIFF_DV)),
        'mla_qa_g': gain((DEPTH, MLA_Q_RANK)),
        'mla_w_uq': nrm((DEPTH, MLA_Q_RANK, MLA_HEADS * (MLA_NOPE + MLA_ROPE)), MLA_Q_RANK ** -0.5),
        'mla_kva_g': gain((DEPTH, MLA_KV_RANK)),
        'mla_w_ukv': nrm((DEPTH, MLA_KV_RANK, MLA_HEADS * (MLA_NOPE + MLA_DV)), MLA_KV_RANK ** -0.5),
        'mla_qn_g': gain((DEPTH, MLA_NOPE + MLA_ROPE)),
        'mla_kn_g': gain((DEPTH, MLA_NOPE + MLA_ROPE)),
        'swa_qn_g': gain((DEPTH, SWA_DH)),
        'swa_kn_g': gain((DEPTH, SWA_DH)),
        'swa_sink': nrm((DEPTH, SWA_HEADS), 0.5),
        'ax_qn_g': gain((DEPTH, AX_DH)),
        'ax_kn_g': gain((DEPTH, AX_DH)),
        'w_branch': nrm((DEPTH, N_BRANCH, BRANCH_W, d), BRANCH_W ** -0.5),
        'w_out': nrm((DEPTH, d, d), d ** -0.5),
        'router_w': nrm((DEPTH, d, N_EXPERTS), d ** -0.5),
        'router_b': nrm((DEPTH, N_EXPERTS), 0.01),
        'w_gu': nrm((DEPTH, N_EXPERTS, d, 2 * D_EXPERT), d ** -0.5),
        'b_gu': nrm((DEPTH, N_EXPERTS, 2 * D_EXPERT), 0.02),
        'w_down': nrm((DEPTH, N_EXPERTS, D_EXPERT, d), D_EXPERT ** -0.5),
        'b_down': nrm((DEPTH, N_EXPERTS, d), 0.02),
    }


def reference(x, c, ctx, c_ctx, ada_w, ada_b, norm1_g, norm2_g, w_in,
              diff_qn_g, diff_kn_g, diff_lambda, diff_subln_g,
              mla_qa_g, mla_w_uq, mla_kva_g, mla_w_ukv, mla_qn_g, mla_kn_g,
              swa_qn_g, swa_kn_g, swa_sink, ax_qn_g, ax_kn_g,
              w_branch, w_out, router_w, router_b, w_gu, b_gu, w_down, b_down):
    b, s, d = x.shape
    n_ctx = ctx.shape[1]
    grid_rows = s // GRID_W
    rows = jnp.repeat(jnp.arange(grid_rows, dtype=jnp.float32), GRID_W)
    cols = jnp.tile(jnp.arange(GRID_W, dtype=jnp.float32), grid_rows)
    diff_rope = rope_tables(rows, cols, DIFF_DK)
    mla_rope = rope_tables(rows, cols, MLA_ROPE)
    swa_rope = rope_tables(rows, cols, SWA_DH)
    ax_rope = rope_tables(rows, cols, AX_DH)
    silu_c = jax.nn.silu(c)
    silu_cc = jax.nn.silu(c_ctx)
    xl, xc = x, ctx
    for l in range(DEPTH):
        with_ctx = l < DEPTH - 1
        lam_init = 0.8 - 0.6 * math.exp(-0.3 * l)
        sh1, sc1, g1, sh2, sc2, g2 = jnp.split((silu_c @ ada_w[l] + ada_b[l])[:, None, :], 6, axis=-1)
        sh1c, sc1c, g1c, sh2c, sc2c, g2c = jnp.split(silu_cc @ ada_w[l] + ada_b[l], 6, axis=-1)
        hl = rms_norm(xl, norm1_g[l]) * (1.0 + sc1) + sh1
        hc = rms_norm(xc, norm1_g[l]) * (1.0 + sc1c) + sh1c
        pl = split_columns(hl @ w_in[l])
        pc = split_columns(hc @ w_in[l])
        oa = diff_branch(pl[0:3], pc[0:3], diff_qn_g[l], diff_kn_g[l], diff_lambda[l], diff_subln_g[l],
                         lam_init, *diff_rope, with_ctx)
        ob = mla_branch(pl[3:6], pc[3:6], mla_qa_g[l], mla_w_uq[l], mla_kva_g[l], mla_w_ukv[l],
                        mla_qn_g[l], mla_kn_g[l], *mla_rope, with_ctx)
        oc = swa_branch(pl[6:9], pc[6:9], swa_qn_g[l], swa_kn_g[l], swa_sink[l], *swa_rope, with_ctx)
        od = axial_branch(pl[9:12], pc[9:12], ax_qn_g[l], ax_kn_g[l], *ax_rope, with_ctx)
        xl = xl + g1 * merge_branches([oa[0], ob[0], oc[0], od[0]], pl[12], w_branch[l], w_out[l])
        h2l = rms_norm(xl, norm2_g[l]) * (1.0 + sc2) + sh2
        moe_args = (router_w[l], router_b[l], w_gu[l], b_gu[l], w_down[l], b_down[l])
        if with_ctx:
            xc = xc + g1c * merge_branches([oa[1], ob[1], oc[1], od[1]], pc[12], w_branch[l], w_out[l])
            h2c = rms_norm(xc, norm2_g[l]) * (1.0 + sc2c) + sh2c
            y = moe_ffn(jnp.concatenate([h2l.reshape(-1, d), h2c.reshape(-1, d)], axis=0), *moe_args)
            xl = xl + g2 * y[:b * s].reshape(b, s, d)
            xc = xc + g2c * y[b * s:].reshape(b, n_ctx, d)
        else:
            xl = xl + g2 * moe_ffn(h2l.reshape(-1, d), *moe_args).reshape(b, s, d)
    return xl
```

```python
import functools
import math

import jax
import jax.numpy as jnp
from jax import lax
from jax.experimental import pallas as pl
from jax.experimental.pallas import tpu as pltpu

F32 = jnp.float32
BF16 = jnp.bfloat16

GRID_W = 64
QBLK = 128
WINDOW = 128
ROPE_BASE = 10000.0
EPS = 1e-6
NEG_INF = -1e30

DIFF_HEADS = 8
DIFF_DK = 64
DIFF_DV = 128
MLA_HEADS = 8
MLA_NOPE = 128
MLA_ROPE = 64
MLA_DV = 128
MLA_Q_RANK = 512
MLA_KV_RANK = 256
MLA_DKP = 256
SWA_HEADS = 8
SWA_KV_HEADS = 2
SWA_DH = 128
AX_HEADS = 8
AX_KV_HEADS = 2
AX_DH = 128
N_BRANCH = 4
BRANCH_W = 1024
N_EXPERTS = 32
TOP_K = 4
D_EXPERT = 768
SWIGLU_LIMIT = 7.0
SWIGLU_ALPHA = 1.702

LANE = 128
ROW_TILE = 512
MOE_BLK = 256
GATE_ALIGN = 512
VMEM_LIMIT = 56 * 1024 * 1024


def _cparams(sem):
    return pltpu.CompilerParams(dimension_semantics=sem, vmem_limit_bytes=VMEM_LIMIT)


def _ada_kernel(a_ref, w_ref, b_ref, o_ref):
    a = a_ref[...].astype(BF16)
    w = w_ref[...].astype(BF16)
    o_ref[...] = jnp.dot(a, w, preferred_element_type=F32) + b_ref[...]


def ada_modulation(a, ada_w, ada_b):
    depth, d, n = ada_w.shape
    tn = math.gcd(n, 1024)
    return pl.pallas_call(
        _ada_kernel,
        grid=(depth, n // tn),
        in_specs=[
            pl.BlockSpec((8, d), lambda l, j: (0, 0)),
            pl.BlockSpec((None, d, tn), lambda l, j: (l, 0, j)),
            pl.BlockSpec((None, 1, tn), lambda l, j: (l, 0, j)),
        ],
        out_specs=pl.BlockSpec((None, 8, tn), lambda l, j: (l, 0, j)),
        out_shape=jax.ShapeDtypeStruct((depth, 8, n), F32),
        compiler_params=_cparams(("arbitrary", "arbitrary")),
        name="ada_modulation",
    )(a, ada_w, ada_b.reshape(depth, 1, n))


def _norm_mm_kernel(x_ref, a_ref, s_ref, w_ref, o_ref, h_ref):
    @pl.when(pl.program_id(1) == 0)
    def _():
        x = x_ref[...].astype(F32)
        ms = jnp.mean(x * x, axis=-1, keepdims=True)
        h = x * lax.rsqrt(ms + EPS) * a_ref[...] + s_ref[...]
        h_ref[...] = h.astype(h_ref.dtype)

    o_ref[...] = jnp.dot(h_ref[...], w_ref[...], preferred_element_type=F32).astype(o_ref.dtype)


def norm_matmul(x, x_col_blk, k, a, s, grp_of_tile, w, tn, out_dtype):
    t = x.shape[0]
    n = w.shape[1]
    tm = ROW_TILE
    return pl.pallas_call(
        _norm_mm_kernel,
        grid=(t // tm, n // tn),
        in_specs=[
            pl.BlockSpec((tm, k), lambda i, j: (i, x_col_blk)),
            pl.BlockSpec((None, 1, k), lambda i, j: (grp_of_tile(i), 0, 0)),
            pl.BlockSpec((None, 1, k), lambda i, j: (grp_of_tile(i), 0, 0)),
            pl.BlockSpec((k, tn), lambda i, j: (0, j)),
        ],
        out_specs=pl.BlockSpec((tm, tn), lambda i, j: (i, j)),
        out_shape=jax.ShapeDtypeStruct((t, n), out_dtype),
        scratch_shapes=[pltpu.VMEM((tm, k), BF16)],
        compiler_params=_cparams(("arbitrary", "arbitrary")),
        name="norm_matmul",
    )(x, a, s, w)


def _softmax_chunk(q, k, v, mask, m_ref, l_ref, acc_ref):
    s = lax.dot_general(q, k, (((1,), (1,)), ((), ())), preferred_element_type=F32)
    if mask is not None:
        s = jnp.where(mask, s, NEG_INF)
    m_prev = m_ref[...]
    m_new = jnp.maximum(m_prev, jnp.max(s, axis=-1, keepdims=True))
    alpha = jnp.exp(m_prev - m_new)
    p = jnp.exp(s - m_new)
    l_ref[...] = alpha * l_ref[...] + jnp.sum(p, axis=-1, keepdims=True)
    acc_ref[...] = alpha * acc_ref[...] + jnp.dot(p.astype(v.dtype), v, preferred_element_type=F32)
    m_ref[...] = m_new


def _init_state(sink_ref, m_ref, l_ref, acc_ref):
    if sink_ref is None:
        m_ref[...] = jnp.full(m_ref.shape, NEG_INF, F32)
        l_ref[...] = jnp.zeros(l_ref.shape, F32)
    else:
        m_ref[...] = sink_ref[...]
        l_ref[...] = jnp.ones(l_ref.shape, F32)
    acc_ref[...] = jnp.zeros(acc_ref.shape, F32)


def _write_out(o_ref, l_ref, acc_ref, r, tq, dv):
    o = acc_ref[...] / l_ref[...]
    for h in range(r):
        o_ref[:, h * dv:(h + 1) * dv] = o[h * tq:(h + 1) * tq].astype(o_ref.dtype)


def _flash_kernel(*refs, r, tq, tk, skv, dv, has_sink, diff):
    refs = list(refs)
    q_ref, k_ref, v_ref = refs[:3]
    pos = 3
    sink_ref = None
    if has_sink:
        sink_ref = refs[pos]
        pos += 1
    if diff:
        lam_ref, sg_ref = refs[pos], refs[pos + 1]
        pos += 2
    o_ref, m_ref, l_ref, acc_ref = refs[pos:pos + 4]

    dk = q_ref.shape[-1]
    q = q_ref[...].reshape(r * tq, dk)
    _init_state(sink_ref, m_ref, l_ref, acc_ref)

    nfull = skv // tk
    rem = skv - nfull * tk

    def body(c, carry):
        off = pl.multiple_of(c * tk, tk)
        _softmax_chunk(q, k_ref[pl.ds(off, tk), :], v_ref[pl.ds(off, tk), :], None, m_ref, l_ref, acc_ref)
        return carry

    if nfull > 0:
        lax.fori_loop(0, nfull, body, 0)
    if rem > 0:
        _softmax_chunk(q, k_ref[pl.ds(nfull * tk, rem), :], v_ref[pl.ds(nfull * tk, rem), :], None,
                       m_ref, l_ref, acc_ref)

    if diff:
        o = acc_ref[...] / l_ref[...]
        od = o[:tq] - lam_ref[0] * o[tq:]
        ms = jnp.mean(od * od, axis=-1, keepdims=True)
        o_ref[...] = (od * lax.rsqrt(ms + EPS) * sg_ref[...]).astype(o_ref.dtype)
    else:
        _write_out(o_ref, l_ref, acc_ref, r, tq, dv)


def flash_attention(q, k, v, *, sq, q_off_blk, skv, kv_off_blk, tq, tk, sink=None, diff=None, out_dtype=BF16):
    b, g, r, _, dk = q.shape
    dv = v.shape[-1]
    m = r * tq
    nq = sq // tq
    in_specs = [
        pl.BlockSpec((None, None, r, tq, dk), lambda bi, gi, i: (bi, gi, 0, i + q_off_blk, 0)),
        pl.BlockSpec((None, None, skv, dk), lambda bi, gi, i: (bi, gi, kv_off_blk, 0)),
        pl.BlockSpec((None, None, skv, dv), lambda bi, gi, i: (bi, gi, kv_off_blk, 0)),
    ]
    args = [q, k, v]
    if sink is not None:
        in_specs.append(pl.BlockSpec((None, m, 1), lambda bi, gi, i: (gi, 0, 0)))
        args.append(sink)
    if diff is not None:
        lam, sg = diff
        in_specs.append(pl.BlockSpec(memory_space=pltpu.SMEM))
        in_specs.append(pl.BlockSpec((1, dv), lambda bi, gi, i: (0, 0)))
        args += [lam, sg]
        out_w = dv
    else:
        out_w = r * dv
    kern = functools.partial(_flash_kernel, r=r, tq=tq, tk=tk, skv=skv, dv=dv,
                             has_sink=sink is not None, diff=diff is not None)
    return pl.pallas_call(
        kern,
        grid=(b, g, nq),
        in_specs=in_specs,
        out_specs=pl.BlockSpec((None, tq, out_w), lambda bi, gi, i: (bi, i, gi)),
        out_shape=jax.ShapeDtypeStruct((b, sq, g * out_w), out_dtype),
        scratch_shapes=[pltpu.VMEM((m, 1), F32), pltpu.VMEM((m, 1), F32), pltpu.VMEM((m, dv), F32)],
        compiler_params=_cparams(("arbitrary", "arbitrary", "arbitrary")),
        name="flash_attention",
    )(*args)


def _window_kernel(q_ref, k_ref, v_ref, sink_ref, o_ref, m_ref, l_ref, acc_ref, *, r, tq, s_lat, n_ctx, win, dv):
    i = pl.program_id(2)
    dk = q_ref.shape[-1]
    m = r * tq
    wlen = tq + 2 * win
    q = q_ref[...].reshape(m, dk)
    _init_state(sink_ref, m_ref, l_ref, acc_ref)

    start = jnp.clip(i * tq - win, 0, s_lat - wlen)
    start = pl.multiple_of(start, win)
    qpos = i * tq + (lax.broadcasted_iota(jnp.int32, (m, wlen), 0) & (tq - 1))
    kpos = start + lax.broadcasted_iota(jnp.int32, (m, wlen), 1)
    ok = jnp.abs(kpos - qpos) <= win
    _softmax_chunk(q, k_ref[pl.ds(start, wlen), :], v_ref[pl.ds(start, wlen), :], ok, m_ref, l_ref, acc_ref)
    _softmax_chunk(q, k_ref[pl.ds(s_lat, n_ctx), :], v_ref[pl.ds(s_lat, n_ctx), :], None, m_ref, l_ref, acc_ref)
    _write_out(o_ref, l_ref, acc_ref, r, tq, dv)


def window_attention(q, k, v, sink, *, s_lat, n_ctx, tq, out_dtype=BF16):
    b, g, r, sall, dk = q.shape
    dv = v.shape[-1]
    m = r * tq
    kern = functools.partial(_window_kernel, r=r, tq=tq, s_lat=s_lat, n_ctx=n_ctx, win=WINDOW, dv=dv)
    return pl.pallas_call(
        kern,
        grid=(b, g, s_lat // tq),
        in_specs=[
            pl.BlockSpec((None, None, r, tq, dk), lambda bi, gi, i: (bi, gi, 0, i, 0)),
            pl.BlockSpec((None, None, sall, dk), lambda bi, gi, i: (bi, gi, 0, 0)),
            pl.BlockSpec((None, None, sall, dv), lambda bi, gi, i: (bi, gi, 0, 0)),
            pl.BlockSpec((None, m, 1), lambda bi, gi, i: (gi, 0, 0)),
        ],
        out_specs=pl.BlockSpec((None, tq, r * dv), lambda bi, gi, i: (bi, i, gi)),
        out_shape=jax.ShapeDtypeStruct((b, s_lat, g * r * dv), out_dtype),
        scratch_shapes=[pltpu.VMEM((m, 1), F32), pltpu.VMEM((m, 1), F32), pltpu.VMEM((m, dv), F32)],
        compiler_params=_cparams(("arbitrary", "arbitrary", "arbitrary")),
        name="window_attention",
    )(q, k, v, sink)


def _merge_kernel(o0, o1, o2, o3, g0, g1, g2, g3, w_ref, y_ref):
    acc = None
    for bi, (o_ref, g_ref) in enumerate(((o0, g0), (o1, g1), (o2, g2), (o3, g3))):
        z = jnp.dot(o_ref[...], w_ref[bi], preferred_element_type=F32)
        z = jax.nn.sigmoid(g_ref[...].astype(F32)) * z
        acc = z if acc is None else acc + z
    y_ref[...] = acc.astype(y_ref.dtype)


def merge_branches(outs, p, gate_col0, w_branch, t_rows):
    d = w_branch.shape[-1]
    bw = w_branch.shape[1]
    tm, tn = ROW_TILE, min(GATE_ALIGN, d)
    g_blk0 = gate_col0 // tn
    per_branch = d // tn

    def gate_spec(bi):
        return pl.BlockSpec((tm, tn), lambda i, j: (i, g_blk0 + bi * per_branch + j))

    return pl.pallas_call(
        _merge_kernel,
        grid=(t_rows // tm, d // tn),
        in_specs=[pl.BlockSpec((tm, bw), lambda i, j: (i, 0)) for _ in range(N_BRANCH)]
        + [gate_spec(bi) for bi in range(N_BRANCH)]
        + [pl.BlockSpec((N_BRANCH, bw, tn), lambda i, j: (0, 0, j))],
        out_specs=pl.BlockSpec((tm, tn), lambda i, j: (i, j)),
        out_shape=jax.ShapeDtypeStruct((t_rows, d), BF16),
        compiler_params=_cparams(("arbitrary", "arbitrary")),
        name="merge_branches",
    )(*outs, p, p, p, p, w_branch)


def _proj_res_kernel(y_ref, w_ref, x_ref, g_ref, o_ref):
    z = jnp.dot(y_ref[...], w_ref[...], preferred_element_type=F32)
    o_ref[...] = x_ref[...] + g_ref[...] * z


def proj_residual(y, w, x, gate, grp_of_tile):
    t, d = y.shape
    tm = ROW_TILE
    return pl.pallas_call(
        _proj_res_kernel,
        grid=(t // tm,),
        in_specs=[
            pl.BlockSpec((tm, d), lambda i: (i, 0)),
            pl.BlockSpec((d, d), lambda i: (0, 0)),
            pl.BlockSpec((tm, d), lambda i: (i, 0)),
            pl.BlockSpec((None, 1, d), lambda i: (grp_of_tile(i), 0, 0)),
        ],
        out_specs=pl.BlockSpec((tm, d), lambda i: (i, 0)),
        out_shape=jax.ShapeDtypeStruct((t, d), F32),
        compiler_params=_cparams(("arbitrary",)),
        name="proj_residual",
    )(y, w, x, gate)


def _moe_kernel(be_ref, nb_ref, x_ref, wgu_ref, bgu_ref, wd_ref, bd_ref, sw_ref, o_ref):
    i = pl.program_id(0)

    @pl.when(i < nb_ref[0])
    def _():
        gu = jnp.dot(x_ref[...], wgu_ref[...], preferred_element_type=F32) + bgu_ref[...]
        gate = jnp.minimum(gu[:, :D_EXPERT], SWIGLU_LIMIT)
        up = jnp.clip(gu[:, D_EXPERT:], -SWIGLU_LIMIT, SWIGLU_LIMIT)
        act = (up + 1.0) * gate * jax.nn.sigmoid(SWIGLU_ALPHA * gate)
        y = jnp.dot(act.astype(BF16), wd_ref[...], preferred_element_type=F32) + bd_ref[...]
        o_ref[...] = (y * sw_ref[...]).astype(o_ref.dtype)

    @pl.when(i >= nb_ref[0])
    def _():
        o_ref[...] = jnp.zeros(o_ref.shape, o_ref.dtype)


def moe_ffn_blocks(blk_expert, n_used, x_sorted, w_gu, b_gu, w_down, b_down, slot_w):
    nslots, d = x_sorted.shape
    bm = MOE_BLK
    nblk = nslots // bm
    ne, _, n2 = w_gu.shape
    de = w_down.shape[1]
    grid_spec = pltpu.PrefetchScalarGridSpec(
        num_scalar_prefetch=2,
        grid=(nblk,),
        in_specs=[
            pl.BlockSpec((bm, d), lambda i, be, nb: (i, 0)),
            pl.BlockSpec((None, d, n2), lambda i, be, nb: (be[i], 0, 0)),
            pl.BlockSpec((None, 1, n2), lambda i, be, nb: (be[i], 0, 0)),
            pl.BlockSpec((None, de, d), lambda i, be, nb: (be[i], 0, 0)),
            pl.BlockSpec((None, 1, d), lambda i, be, nb: (be[i], 0, 0)),
            pl.BlockSpec((bm, 1), lambda i, be, nb: (i, 0)),
        ],
        out_specs=pl.BlockSpec((bm, d), lambda i, be, nb: (i, 0)),
    )
    return pl.pallas_call(
        _moe_kernel,
        grid_spec=grid_spec,
        out_shape=jax.ShapeDtypeStruct((nslots, d), BF16),
        compiler_params=_cparams(("arbitrary",)),
        name="moe_ffn",
    )(blk_expert, n_used, x_sorted, w_gu, b_gu.reshape(ne, 1, n2), w_down, b_down.reshape(ne, 1, d), slot_w)


def _rms(x, g):
    xf = x.astype(F32)
    return xf * lax.rsqrt(jnp.mean(xf * xf, axis=-1, keepdims=True) + EPS) * g.astype(F32)


def _rope_tables(rows, cols, dim, n_ctx):
    quarter = dim // 4
    inv = ROPE_BASE ** (-jnp.arange(quarter, dtype=F32) / quarter)
    ang = jnp.concatenate([rows[:, None] * inv, cols[:, None] * inv], axis=-1)
    cos = jnp.concatenate([jnp.cos(ang), jnp.ones((n_ctx, dim // 2), F32)], axis=0)
    sin = jnp.concatenate([jnp.sin(ang), jnp.zeros((n_ctx, dim // 2), F32)], axis=0)
    return cos, sin


def _rope(x, cos, sin):
    half = x.shape[-1] // 2
    x1, x2 = x[..., :half], x[..., half:]
    c = cos[None, :, None, :]
    s = sin[None, :, None, :]
    return jnp.concatenate([x1 * c - x2 * s, x1 * s + x2 * c], axis=-1)


def _w_in_layout(d):
    sizes = (
        ("diff_q", DIFF_HEADS * 2 * DIFF_DK), ("diff_k", DIFF_HEADS * 2 * DIFF_DK), ("diff_v", DIFF_HEADS * DIFF_DV),
        ("mla_cq", MLA_Q_RANK), ("mla_ckv", MLA_KV_RANK), ("mla_kr", MLA_ROPE),
        ("swa_q", SWA_HEADS * SWA_DH), ("swa_k", SWA_KV_HEADS * SWA_DH), ("swa_v", SWA_KV_HEADS * SWA_DH),
        ("ax_q", AX_HEADS * AX_DH), ("ax_k", AX_KV_HEADS * AX_DH), ("ax_v", AX_KV_HEADS * AX_DH),
        ("gates", N_BRANCH * d),
    )
    src, dst, out = 0, 0, {}
    for name, n in sizes:
        align = GATE_ALIGN if name == "gates" else LANE
        dst = -(-dst // align) * align
        out[name] = (src, dst, n)
        src += n
        dst += n
    return out, src, dst


def _pack_w_in(w_in_l, layout, n_packed_pad):
    d = w_in_l.shape[0]
    cols = []
    pos = 0
    for name, (src, dst, n) in layout.items():
        if dst > pos:
            cols.append(jnp.zeros((d, dst - pos), BF16))
        cols.append(w_in_l[:, src:src + n].astype(BF16))
        pos = dst + n
    if n_packed_pad > pos:
        cols.append(jnp.zeros((d, n_packed_pad - pos), BF16))
    return jnp.concatenate(cols, axis=1)


def kernel(x, c, ctx, c_ctx, ada_w, ada_b, norm1_g, norm2_g, w_in, diff_qn_g, diff_kn_g, diff_lambda, diff_subln_g, mla_qa_g, mla_w_uq, mla_kva_g, mla_w_ukv, mla_qn_g, mla_kn_g, swa_qn_g, swa_kn_g, swa_sink, ax_qn_g, ax_kn_g, w_branch, w_out, router_w, router_b, w_gu, b_gu, w_down, b_down):
    b, s, d = x.shape
    n_ctx = ctx.shape[1]
    depth = ada_w.shape[0]
    sall = s + n_ctx
    t_lat = b * s
    t_all = t_lat + b * n_ctx
    tm = ROW_TILE
    assert s % tm == 0 and (b * n_ctx) % tm == 0 and b + 1 <= 8
    tiles_per_batch = s // tm

    def grp_of_tile(i):
        return jnp.minimum(i // tiles_per_batch, b)

    grid_rows = s // GRID_W
    rows = jnp.repeat(jnp.arange(grid_rows, dtype=F32), GRID_W)
    cols = jnp.tile(jnp.arange(GRID_W, dtype=F32), grid_rows)
    cos64, sin64 = _rope_tables(rows, cols, DIFF_DK, n_ctx)
    cos128, sin128 = _rope_tables(rows, cols, SWA_DH, n_ctx)

    cond = jnp.concatenate([jax.nn.silu(c), jax.nn.silu(c_ctx)[None, :], jnp.zeros((8 - b - 1, d), F32)], axis=0)
    mod = ada_modulation(cond, ada_w, ada_b)[:, :b + 1]
    mod = mod.reshape(depth, b + 1, 6, d)

    layout, n_src, n_packed = _w_in_layout(d)
    tn_in = 1024
    n_packed_pad = -(-n_packed // tn_in) * tn_in

    def to_bs(a):
        cw = a.shape[-1]
        return jnp.concatenate([a[:t_lat].reshape(b, s, cw), a[t_lat:].reshape(b, n_ctx, cw)], axis=1)

    def to_rows(lat, cx):
        return jnp.concatenate([lat.reshape(t_lat, -1), cx.reshape(b * n_ctx, -1)], axis=0)

    xs = jnp.concatenate([x.reshape(t_lat, d), ctx.reshape(b * n_ctx, d)], axis=0)

    for l in range(depth):
        with_ctx = l < depth - 1
        lam_init = 0.8 - 0.6 * math.exp(-0.3 * l)
        sh1, sc1, g1, sh2, sc2, g2 = [mod[l, :, i] for i in range(6)]

        a1 = (norm1_g[l][None, :] * (1.0 + sc1))[:, None, :]
        p = norm_matmul(xs, 0, d, a1, sh1[:, None, :], grp_of_tile,
                        _pack_w_in(w_in[l], layout, n_packed_pad), tn_in, BF16)

        def seg(name):
            _, dst, n = layout[name]
            return p[:, dst:dst + n]

        scale = DIFF_DK ** -0.5
        q = _rope(_rms(to_bs(seg("diff_q")).reshape(b, sall, 2 * DIFF_HEADS, DIFF_DK), diff_qn_g[l]), cos64, sin64) * scale
        k = _rope(_rms(to_bs(seg("diff_k")).reshape(b, sall, 2 * DIFF_HEADS, DIFF_DK), diff_kn_g[l]), cos64, sin64)
        q = q.reshape(b, sall, DIFF_HEADS, 2, DIFF_DK)
        zq = jnp.zeros_like(q[:, :, :, 0])
        qa = jnp.concatenate([q[:, :, :, 0], zq], axis=-1)
        qb = jnp.concatenate([zq, q[:, :, :, 1]], axis=-1)
        q = jnp.stack([qa, qb], axis=2).transpose(0, 3, 2, 1, 4).astype(BF16)
        k = k.reshape(b, sall, DIFF_HEADS, 2 * DIFF_DK).transpose(0, 2, 1, 3).astype(BF16)
        v = to_bs(seg("diff_v")).reshape(b, sall, DIFF_HEADS, DIFF_DV).transpose(0, 2, 1, 3)
        lf = diff_lambda[l].astype(F32)
        lam_val = (jnp.exp(jnp.sum(lf[0] * lf[1])) - jnp.exp(jnp.sum(lf[2] * lf[3])) + lam_init).reshape(1)
        sg = (diff_subln_g[l] * (1.0 - lam_init)).reshape(1, DIFF_DV)
        oa_l = flash_attention(q, k, v, sq=s, q_off_blk=0, skv=sall, kv_off_blk=0, tq=256, tk=768, diff=(lam_val, sg))
        if with_ctx:
            oa_c = flash_attention(q, k, v, sq=n_ctx, q_off_blk=s // n_ctx, skv=n_ctx, kv_off_blk=s // n_ctx,
                                   tq=n_ctx, tk=n_ctx, diff=(lam_val, sg))

        scale = (MLA_NOPE + MLA_ROPE) ** -0.5
        dkq = MLA_NOPE + MLA_ROPE
        cq_blk = layout["mla_cq"][1] // MLA_Q_RANK
        q_up = norm_matmul(p, cq_blk, MLA_Q_RANK, mla_qa_g[l].reshape(1, 1, -1), jnp.zeros((1, 1, MLA_Q_RANK), F32),
                           lambda i: 0, mla_w_uq[l].astype(BF16), MLA_HEADS * dkq, BF16)
        ckv_blk = layout["mla_ckv"][1] // MLA_KV_RANK
        kv_up = norm_matmul(p, ckv_blk, MLA_KV_RANK, mla_kva_g[l].reshape(1, 1, -1), jnp.zeros((1, 1, MLA_KV_RANK), F32),
                            lambda i: 0, mla_w_ukv[l].astype(BF16), 1024, BF16)
        q = _rms(to_bs(q_up).reshape(b, sall, MLA_HEADS, dkq), mla_qn_g[l])
        q = jnp.concatenate([q[..., :MLA_NOPE], _rope(q[..., MLA_NOPE:], cos64, sin64)], axis=-1) * scale
        q = jnp.pad(q, ((0, 0), (0, 0), (0, 0), (0, MLA_DKP - dkq)))
        q = q.transpose(0, 2, 1, 3)[:, :, None].astype(BF16)
        kv = to_bs(kv_up).reshape(b, sall, MLA_HEADS, MLA_NOPE + MLA_DV)
        kr = to_bs(seg("mla_kr"))
        k_rope = jnp.broadcast_to(kr[:, :, None, :], (b, sall, MLA_HEADS, MLA_ROPE))
        k = _rms(jnp.concatenate([kv[..., :MLA_NOPE], k_rope], axis=-1), mla_kn_g[l])
        k = jnp.concatenate([k[..., :MLA_NOPE], _rope(k[..., MLA_NOPE:], cos64, sin64)], axis=-1)
        k = jnp.pad(k, ((0, 0), (0, 0), (0, 0), (0, MLA_DKP - dkq))).transpose(0, 2, 1, 3).astype(BF16)
        v = kv[..., MLA_NOPE:].transpose(0, 2, 1, 3)
        ob_l = flash_attention(q, k, v, sq=s, q_off_blk=0, skv=sall, kv_off_blk=0, tq=512, tk=768)
        if with_ctx:
            ob_c = flash_attention(q, k, v, sq=n_ctx, q_off_blk=s // n_ctx, skv=n_ctx, kv_off_blk=s // n_ctx,
                                   tq=n_ctx, tk=n_ctx)

        def gqa(prefix, n_heads, n_kv, dh, qn, kn):
            rr = n_heads // n_kv
            qq = _rope(_rms(to_bs(seg(prefix + "_q")).reshape(b, sall, n_heads, dh), qn), cos128, sin128) * dh ** -0.5
            kk = _rope(_rms(to_bs(seg(prefix + "_k")).reshape(b, sall, n_kv, dh), kn), cos128, sin128)
            vv = to_bs(seg(prefix + "_v")).reshape(b, sall, n_kv, dh)
            qq = qq.reshape(b, sall, n_kv, rr, dh).transpose(0, 2, 3, 1, 4).astype(BF16)
            return qq, kk.transpose(0, 2, 1, 3).astype(BF16), vv.transpose(0, 2, 1, 3)

        rr = SWA_HEADS // SWA_KV_HEADS
        q, k, v = gqa("swa", SWA_HEADS, SWA_KV_HEADS, SWA_DH, swa_qn_g[l], swa_kn_g[l])
        sink_gr = swa_sink[l].astype(F32).reshape(SWA_KV_HEADS, rr)
        tq_w = 256
        oc_l = window_attention(q, k, v, jnp.repeat(sink_gr, tq_w, axis=1)[:, :, None], s_lat=s, n_ctx=n_ctx, tq=tq_w)
        if with_ctx:
            oc_c = flash_attention(q, k, v, sq=n_ctx, q_off_blk=s // n_ctx, skv=n_ctx, kv_off_blk=s // n_ctx,
                                   tq=n_ctx, tk=n_ctx, sink=jnp.repeat(sink_gr, n_ctx, axis=1)[:, :, None])

        q, k, v = gqa("ax", AX_HEADS, AX_KV_HEADS, AX_DH, ax_qn_g[l], ax_kn_g[l])
        od_l = flash_attention(q, k, v, sq=s, q_off_blk=0, skv=sall, kv_off_blk=0, tq=128, tk=768)
        if with_ctx:
            od_c = flash_attention(q, k, v, sq=n_ctx, q_off_blk=s // n_ctx, skv=n_ctx, kv_off_blk=s // n_ctx,
                                   tq=n_ctx, tk=n_ctx)

        if with_ctx:
            outs = [to_rows(ol, oc) for ol, oc in ((oa_l, oa_c), (ob_l, ob_c), (oc_l, oc_c), (od_l, od_c))]
            t_rows = t_all
        else:
            outs = [ol.reshape(t_lat, BRANCH_W) for ol in (oa_l, ob_l, oc_l, od_l)]
            t_rows = t_lat
            xs = xs[:t_lat]
        y = merge_branches(outs, p, layout["gates"][1], w_branch[l].astype(BF16), t_rows)
        xs = proj_residual(y, w_out[l].astype(BF16), xs, g1[:, None, :], grp_of_tile)

        n_tiles = t_rows // tm
        grp_rows = jnp.repeat(grp_of_tile(jnp.arange(n_tiles)), tm)
        h2 = _rms(xs, norm2_g[l]) * (1.0 + sc2)[grp_rows] + sh2[grp_rows]
        logits = jnp.dot(h2, router_w[l], precision=lax.Precision.HIGHEST) + router_b[l].astype(F32)
        top_val, top_idx = lax.top_k(logits, TOP_K)
        top_w = jax.nn.softmax(top_val, axis=-1)
        n_assign = t_rows * TOP_K
        bm = MOE_BLK
        flat_e = top_idx.reshape(-1)
        order = jnp.argsort(flat_e)
        sorted_e = flat_e[order]
        counts = jnp.bincount(flat_e, length=N_EXPERTS)
        padded = (counts + bm - 1) // bm * bm
        pad_end = jnp.cumsum(padded)
        pad_start = pad_end - padded
        start = jnp.cumsum(counts) - counts
        dest = (pad_start[sorted_e] + jnp.arange(n_assign) - start[sorted_e]).astype(jnp.int32)
        n_blocks = -(-n_assign // bm) + N_EXPERTS
        slot_tok = jnp.zeros((n_blocks * bm,), jnp.int32).at[dest].set((order // TOP_K).astype(jnp.int32))
        slot_w = jnp.zeros((n_blocks * bm,), F32).at[dest].set(top_w.reshape(-1)[order])
        blk_expert = jnp.minimum(jnp.searchsorted(pad_end, jnp.arange(n_blocks) * bm, side='right'),
                                 N_EXPERTS - 1).astype(jnp.int32)
        n_used = (pad_end[-1] // bm).astype(jnp.int32).reshape(1)
        slot_of = jnp.zeros((n_assign,), jnp.int32).at[order].set(dest).reshape(t_rows, TOP_K)
        x_sorted = h2.astype(BF16)[slot_tok]
        yb = moe_ffn_blocks(blk_expert, n_used, x_sorted, w_gu[l].astype(BF16), b_gu[l], w_down[l].astype(BF16),
                            b_down[l], slot_w[:, None])
        y_tok = jnp.sum(yb[slot_of].astype(F32), axis=1)
        xs = xs + g2[grp_rows] * y_tok

    return xs[:t_lat].reshape(b, s, d)
```

```python
import functools
import math

import jax
import jax.numpy as jnp
from jax import lax
from jax.experimental import pallas as pl
from jax.experimental.pallas import tpu as pltpu

F32 = jnp.float32
BF16 = jnp.bfloat16

GRID_W = 64
QBLK = 128
WINDOW = 128
ROPE_BASE = 10000.0
EPS = 1e-6
NEG_INF = -1e30

DIFF_HEADS = 8
DIFF_DK = 64
DIFF_DV = 128
MLA_HEADS = 8
MLA_NOPE = 128
MLA_ROPE = 64
MLA_DV = 128
MLA_Q_RANK = 512
MLA_KV_RANK = 256
MLA_DKP = 256
SWA_HEADS = 8
SWA_KV_HEADS = 2
SWA_DH = 128
AX_HEADS = 8
AX_KV_HEADS = 2
AX_DH = 128
N_BRANCH = 4
BRANCH_W = 1024
N_EXPERTS = 32
TOP_K = 4
D_EXPERT = 768
SWIGLU_LIMIT = 7.0
SWIGLU_ALPHA = 1.702

LANE = 128
ROW_TILE = 512
MOE_BLK = 256
CHAIN_W = 512
KV_CHUNK = 768
LOG2E = math.log2(math.e)
GATE_ALIGN = 512
VMEM_LIMIT = 56 * 1024 * 1024


def _cparams(sem):
    return pltpu.CompilerParams(dimension_semantics=sem, vmem_limit_bytes=VMEM_LIMIT)


def _ada_kernel(a_ref, w_ref, b_ref, o_ref):
    a = a_ref[...].astype(BF16)
    w = w_ref[...].astype(BF16)
    o_ref[...] = jnp.dot(a, w, preferred_element_type=F32) + b_ref[...]


def ada_modulation(a, ada_w, ada_b):
    depth, d, n = ada_w.shape
    tn = math.gcd(n, 1024)
    return pl.pallas_call(
        _ada_kernel,
        grid=(depth, n // tn),
        in_specs=[
            pl.BlockSpec((8, d), lambda l, j: (0, 0)),
            pl.BlockSpec((None, d, tn), lambda l, j: (l, 0, j)),
            pl.BlockSpec((None, 1, tn), lambda l, j: (l, 0, j)),
        ],
        out_specs=pl.BlockSpec((None, 8, tn), lambda l, j: (l, 0, j)),
        out_shape=jax.ShapeDtypeStruct((depth, 8, n), F32),
        compiler_params=_cparams(("arbitrary", "arbitrary")),
        name="ada_modulation",
    )(a, ada_w, ada_b.reshape(depth, 1, n))


def _norm_mm_kernel(x_ref, a_ref, s_ref, w_ref, o_ref, h_ref):
    @pl.when(pl.program_id(1) == 0)
    def _():
        x = x_ref[...].astype(F32)
        ms = jnp.mean(x * x, axis=-1, keepdims=True)
        h = x * lax.rsqrt(ms + EPS) * a_ref[...] + s_ref[...]
        h_ref[...] = h.astype(h_ref.dtype)

    o_ref[...] = jnp.dot(h_ref[...], w_ref[...], preferred_element_type=F32).astype(o_ref.dtype)


def norm_matmul(x, x_col_blk, k, a, s, grp_of_tile, w, tn, out_dtype):
    t = x.shape[0]
    n = w.shape[1]
    tm = ROW_TILE
    return pl.pallas_call(
        _norm_mm_kernel,
        grid=(t // tm, n // tn),
        in_specs=[
            pl.BlockSpec((tm, k), lambda i, j: (i, x_col_blk)),
            pl.BlockSpec((None, 1, k), lambda i, j: (grp_of_tile(i), 0, 0)),
            pl.BlockSpec((None, 1, k), lambda i, j: (grp_of_tile(i), 0, 0)),
            pl.BlockSpec((k, tn), lambda i, j: (0, j)),
        ],
        out_specs=pl.BlockSpec((tm, tn), lambda i, j: (i, j)),
        out_shape=jax.ShapeDtypeStruct((t, n), out_dtype),
        scratch_shapes=[pltpu.VMEM((tm, k), BF16)],
        compiler_params=_cparams(("arbitrary", "arbitrary")),
        name="norm_matmul",
    )(x, a, s, w)


def _chain_width(m):
    return min(CHAIN_W, m)


def _kv_chunk(skv):
    return max(t for t in range(LANE, KV_CHUNK + 1, LANE) if skv % t == 0)


def _chain_q(q_ref, h, tq, w):
    if w >= tq:
        n = w // tq
        return q_ref[h * n:(h + 1) * n].reshape(w, q_ref.shape[-1])
    r, t0 = divmod(h * w, tq)
    return q_ref[r, t0:t0 + w, :]


def _scores(kc, qh):
    return lax.dot_general(kc, qh, (((1,), (1,)), ((), ())), preferred_element_type=F32)


def _softmax_pv(st, vtc, mask, m_ref, l_ref, acc_ref, h):
    if mask is not None:
        st = jnp.where(mask, st, NEG_INF)
    m_prev = m_ref[h]
    m_new = jnp.maximum(m_prev, jnp.max(st, axis=0, keepdims=True))
    alpha = jnp.exp2(m_prev - m_new)
    pt = jnp.exp2(st - m_new)
    l_ref[h] = alpha * l_ref[h] + jnp.sum(pt, axis=0, keepdims=True)
    acc_ref[h] = alpha * acc_ref[h] + jnp.dot(vtc, pt.astype(BF16), preferred_element_type=F32)
    m_ref[h] = m_new


def _init_state(sink_ref, m_ref, l_ref, acc_ref):
    if sink_ref is None:
        m_ref[...] = jnp.full(m_ref.shape, NEG_INF, F32)
        l_ref[...] = jnp.zeros(l_ref.shape, F32)
    else:
        m_ref[...] = sink_ref[...]
        l_ref[...] = jnp.ones(l_ref.shape, F32)
    acc_ref[...] = jnp.zeros(acc_ref.shape, F32)


def _chain_out(l_ref, acc_ref, h):
    return (acc_ref[h] / l_ref[h]).T


def _write_out(o_ref, l_ref, acc_ref, nch, tq, w, dv):
    piece = min(w, tq)
    for h in range(nch):
        o = _chain_out(l_ref, acc_ref, h)
        for j in range(w // piece):
            r, t0 = divmod(h * w + j * piece, tq)
            o_ref[t0:t0 + piece, r * dv:(r + 1) * dv] = o[j * piece:(j + 1) * piece].astype(o_ref.dtype)


def _flash_kernel(*refs, r, tq, tk, skv, dv, w, has_sink, diff):
    refs = list(refs)
    q_ref, k_ref, vt_ref = refs[:3]
    pos = 3
    sink_ref = None
    if has_sink:
        sink_ref = refs[pos]
        pos += 1
    if diff:
        lam_ref, sg_ref = refs[pos], refs[pos + 1]
        pos += 2
    o_ref, m_ref, l_ref, acc_ref, sa_ref, sb_ref = refs[pos:pos + 6]
    nch = (r * tq) // w
    _init_state(sink_ref, m_ref, l_ref, acc_ref)

    def step(off, off_next, s_cur, s_next):
        vtc = vt_ref[:, pl.ds(off, tk)]
        kn = None if off_next is None else k_ref[pl.ds(off_next, tk), :]
        for h in range(nch):
            if kn is not None:
                s_next[h] = _scores(kn, _chain_q(q_ref, h, tq, w))
            _softmax_pv(s_cur[h], vtc, None, m_ref, l_ref, acc_ref, h)

    n = skv // tk
    k0 = k_ref[pl.ds(0, tk), :]
    for h in range(nch):
        sa_ref[h] = _scores(k0, _chain_q(q_ref, h, tq, w))
    pairs = (n - 1) // 2

    def body(j, carry):
        o0 = pl.multiple_of(2 * j * tk, tk)
        o1 = pl.multiple_of(o0 + tk, tk)
        o2 = pl.multiple_of(o0 + 2 * tk, tk)
        step(o0, o1, sa_ref, sb_ref)
        step(o1, o2, sb_ref, sa_ref)
        return carry

    if pairs > 0:
        lax.fori_loop(0, pairs, body, 0)
    c = 2 * pairs
    if n - c == 2:
        step(c * tk, (c + 1) * tk, sa_ref, sb_ref)
        step((c + 1) * tk, None, sb_ref, sa_ref)
    else:
        step(c * tk, None, sa_ref, sb_ref)

    if diff:
        od = _chain_out(l_ref, acc_ref, 0) - lam_ref[0] * _chain_out(l_ref, acc_ref, 1)
        ms = jnp.mean(od * od, axis=-1, keepdims=True)
        o_ref[...] = (od * lax.rsqrt(ms + EPS) * sg_ref[...]).astype(o_ref.dtype)
    else:
        _write_out(o_ref, l_ref, acc_ref, nch, tq, w, dv)


def _attn_scratch(m, w, dv):
    nch = m // w
    return [pltpu.VMEM((nch, 1, w), F32), pltpu.VMEM((nch, 1, w), F32), pltpu.VMEM((nch, dv, w), F32)]


def flash_attention(q, k, vt, *, sq, q_off_blk, skv, kv_off_blk, tq, tk, sink=None, diff=None, out_dtype=BF16):
    b, g, r, _, dk = q.shape
    dv = vt.shape[2]
    m = r * tq
    w = tq if diff is not None else _chain_width(m)
    assert m % w == 0 and skv % tk == 0 and (diff is None or r == 2)
    nch = m // w
    nq = sq // tq
    in_specs = [
        pl.BlockSpec((None, None, r, tq, dk), lambda bi, gi, i: (bi, gi, 0, i + q_off_blk, 0)),
        pl.BlockSpec((None, None, skv, dk), lambda bi, gi, i: (bi, gi, kv_off_blk, 0)),
        pl.BlockSpec((None, None, dv, skv), lambda bi, gi, i: (bi, gi, 0, kv_off_blk)),
    ]
    args = [q, k, vt]
    if sink is not None:
        in_specs.append(pl.BlockSpec((None, nch, 1, w), lambda bi, gi, i: (gi, 0, 0, 0)))
        args.append(sink)
    if diff is not None:
        lam, sg = diff
        in_specs.append(pl.BlockSpec(memory_space=pltpu.SMEM))
        in_specs.append(pl.BlockSpec((1, dv), lambda bi, gi, i: (0, 0)))
        args += [lam, sg]
        out_w = dv
    else:
        out_w = r * dv
    kern = functools.partial(_flash_kernel, r=r, tq=tq, tk=tk, skv=skv, dv=dv, w=w,
                             has_sink=sink is not None, diff=diff is not None)
    return pl.pallas_call(
        kern,
        grid=(b, g, nq),
        in_specs=in_specs,
        out_specs=pl.BlockSpec((None, tq, out_w), lambda bi, gi, i: (bi, i, gi)),
        out_shape=jax.ShapeDtypeStruct((b, sq, g * out_w), out_dtype),
        scratch_shapes=_attn_scratch(m, w, dv) + [pltpu.VMEM((nch, tk, w), F32), pltpu.VMEM((nch, tk, w), F32)],
        compiler_params=_cparams(("arbitrary", "arbitrary", "arbitrary")),
        name="flash_attention",
    )(*args)


def _window_kernel(q_ref, k_ref, vt_ref, sink_ref, o_ref, m_ref, l_ref, acc_ref, *, r, tq, s_lat, n_ctx, win, dv):
    i = pl.program_id(2)
    w = _chain_width(r * tq)
    nch = (r * tq) // w
    wlen = tq + 2 * win
    _init_state(sink_ref, m_ref, l_ref, acc_ref)

    start = pl.multiple_of(jnp.clip(i * tq - win, 0, s_lat - wlen), win)
    kpos = start + lax.broadcasted_iota(jnp.int32, (wlen, w), 0)
    col = lax.broadcasted_iota(jnp.int32, (wlen, w), 1)
    kw, vtw = k_ref[pl.ds(start, wlen), :], vt_ref[:, pl.ds(start, wlen)]
    kc, vtc = k_ref[pl.ds(s_lat, n_ctx), :], vt_ref[:, pl.ds(s_lat, n_ctx)]
    qs = [_chain_q(q_ref, h, tq, w) for h in range(nch)]
    s_win = [_scores(kw, qh) for qh in qs]
    s_ctx = [_scores(kc, qh) for qh in qs]
    for h in range(nch):
        qpos = i * tq + ((h * w + col) & (tq - 1))
        ok = jnp.abs(kpos - qpos) <= win
        _softmax_pv(s_win[h], vtw, ok, m_ref, l_ref, acc_ref, h)
        _softmax_pv(s_ctx[h], vtc, None, m_ref, l_ref, acc_ref, h)
    _write_out(o_ref, l_ref, acc_ref, nch, tq, w, dv)


def window_attention(q, k, vt, sink, *, s_lat, n_ctx, tq, out_dtype=BF16):
    b, g, r, sall, dk = q.shape
    dv = vt.shape[2]
    m = r * tq
    w = _chain_width(m)
    nch = m // w
    kern = functools.partial(_window_kernel, r=r, tq=tq, s_lat=s_lat, n_ctx=n_ctx, win=WINDOW, dv=dv)
    return pl.pallas_call(
        kern,
        grid=(b, g, s_lat // tq),
        in_specs=[
            pl.BlockSpec((None, None, r, tq, dk), lambda bi, gi, i: (bi, gi, 0, i, 0)),
            pl.BlockSpec((None, None, sall, dk), lambda bi, gi, i: (bi, gi, 0, 0)),
            pl.BlockSpec((None, None, dv, sall), lambda bi, gi, i: (bi, gi, 0, 0)),
            pl.BlockSpec((None, nch, 1, w), lambda bi, gi, i: (gi, 0, 0, 0)),
        ],
        out_specs=pl.BlockSpec((None, tq, r * dv), lambda bi, gi, i: (bi, i, gi)),
        out_shape=jax.ShapeDtypeStruct((b, s_lat, g * r * dv), out_dtype),
        scratch_shapes=_attn_scratch(m, w, dv),
        compiler_params=_cparams(("arbitrary", "arbitrary", "arbitrary")),
        name="window_attention",
    )(q, k, vt, sink)


def _merge_kernel(o0, o1, o2, o3, g0, g1, g2, g3, w_ref, y_ref):
    acc = None
    for bi, (o_ref, g_ref) in enumerate(((o0, g0), (o1, g1), (o2, g2), (o3, g3))):
        z = jnp.dot(o_ref[...], w_ref[bi], preferred_element_type=F32)
        z = jax.nn.sigmoid(g_ref[...].astype(F32)) * z
        acc = z if acc is None else acc + z
    y_ref[...] = acc.astype(y_ref.dtype)


def merge_branches(outs, p, gate_col0, w_branch, t_rows):
    d = w_branch.shape[-1]
    bw = w_branch.shape[1]
    tm, tn = ROW_TILE, min(GATE_ALIGN, d)
    g_blk0 = gate_col0 // tn
    per_branch = d // tn

    def gate_spec(bi):
        return pl.BlockSpec((tm, tn), lambda i, j: (i, g_blk0 + bi * per_branch + j))

    return pl.pallas_call(
        _merge_kernel,
        grid=(t_rows // tm, d // tn),
        in_specs=[pl.BlockSpec((tm, bw), lambda i, j: (i, 0)) for _ in range(N_BRANCH)]
        + [gate_spec(bi) for bi in range(N_BRANCH)]
        + [pl.BlockSpec((N_BRANCH, bw, tn), lambda i, j: (0, 0, j))],
        out_specs=pl.BlockSpec((tm, tn), lambda i, j: (i, j)),
        out_shape=jax.ShapeDtypeStruct((t_rows, d), BF16),
        compiler_params=_cparams(("arbitrary", "arbitrary")),
        name="merge_branches",
    )(*outs, p, p, p, p, w_branch)


def _proj_res_kernel(y_ref, w_ref, x_ref, g_ref, o_ref):
    z = jnp.dot(y_ref[...], w_ref[...], preferred_element_type=F32)
    o_ref[...] = x_ref[...] + g_ref[...] * z


def proj_residual(y, w, x, gate, grp_of_tile):
    t, d = y.shape
    tm = ROW_TILE
    return pl.pallas_call(
        _proj_res_kernel,
        grid=(t // tm,),
        in_specs=[
            pl.BlockSpec((tm, d), lambda i: (i, 0)),
            pl.BlockSpec((d, d), lambda i: (0, 0)),
            pl.BlockSpec((tm, d), lambda i: (i, 0)),
            pl.BlockSpec((None, 1, d), lambda i: (grp_of_tile(i), 0, 0)),
        ],
        out_specs=pl.BlockSpec((tm, d), lambda i: (i, 0)),
        out_shape=jax.ShapeDtypeStruct((t, d), F32),
        compiler_params=_cparams(("arbitrary",)),
        name="proj_residual",
    )(y, w, x, gate)


def _moe_kernel(be_ref, nb_ref, x_ref, wgu_ref, bgu_ref, wd_ref, bd_ref, sw_ref, o_ref):
    i = pl.program_id(0)

    @pl.when(i < nb_ref[0])
    def _():
        gu = jnp.dot(x_ref[...], wgu_ref[...], preferred_element_type=F32) + bgu_ref[...]
        gate = jnp.minimum(gu[:, :D_EXPERT], SWIGLU_LIMIT)
        up = jnp.clip(gu[:, D_EXPERT:], -SWIGLU_LIMIT, SWIGLU_LIMIT)
        act = (up + 1.0) * gate * jax.nn.sigmoid(SWIGLU_ALPHA * gate)
        y = jnp.dot(act.astype(BF16), wd_ref[...], preferred_element_type=F32) + bd_ref[...]
        o_ref[...] = (y * sw_ref[...]).astype(o_ref.dtype)

    @pl.when(i >= nb_ref[0])
    def _():
        o_ref[...] = jnp.zeros(o_ref.shape, o_ref.dtype)


def moe_ffn_blocks(blk_expert, n_used, x_sorted, w_gu, b_gu, w_down, b_down, slot_w):
    nslots, d = x_sorted.shape
    bm = MOE_BLK
    nblk = nslots // bm
    ne, _, n2 = w_gu.shape
    de = w_down.shape[1]
    grid_spec = pltpu.PrefetchScalarGridSpec(
        num_scalar_prefetch=2,
        grid=(nblk,),
        in_specs=[
            pl.BlockSpec((bm, d), lambda i, be, nb: (i, 0)),
            pl.BlockSpec((None, d, n2), lambda i, be, nb: (be[i], 0, 0)),
            pl.BlockSpec((None, 1, n2), lambda i, be, nb: (be[i], 0, 0)),
            pl.BlockSpec((None, de, d), lambda i, be, nb: (be[i], 0, 0)),
            pl.BlockSpec((None, 1, d), lambda i, be, nb: (be[i], 0, 0)),
            pl.BlockSpec((bm, 1), lambda i, be, nb: (i, 0)),
        ],
        out_specs=pl.BlockSpec((bm, d), lambda i, be, nb: (i, 0)),
    )
    return pl.pallas_call(
        _moe_kernel,
        grid_spec=grid_spec,
        out_shape=jax.ShapeDtypeStruct((nslots, d), BF16),
        compiler_params=_cparams(("arbitrary",)),
        name="moe_ffn",
    )(blk_expert, n_used, x_sorted, w_gu, b_gu.reshape(ne, 1, n2), w_down, b_down.reshape(ne, 1, d), slot_w)


def _rms(x, g):
    xf = x.astype(F32)
    return xf * lax.rsqrt(jnp.mean(xf * xf, axis=-1, keepdims=True) + EPS) * g.astype(F32)


def _rope_tables(rows, cols, dim, n_ctx):
    quarter = dim // 4
    inv = ROPE_BASE ** (-jnp.arange(quarter, dtype=F32) / quarter)
    ang = jnp.concatenate([rows[:, None] * inv, cols[:, None] * inv], axis=-1)
    cos = jnp.concatenate([jnp.cos(ang), jnp.ones((n_ctx, dim // 2), F32)], axis=0)
    sin = jnp.concatenate([jnp.sin(ang), jnp.zeros((n_ctx, dim // 2), F32)], axis=0)
    return cos, sin


def _rope(x, cos, sin):
    half = x.shape[-1] // 2
    x1, x2 = x[..., :half], x[..., half:]
    c = cos[None, :, None, :]
    s = sin[None, :, None, :]
    return jnp.concatenate([x1 * c - x2 * s, x1 * s + x2 * c], axis=-1)


def _w_in_layout(d):
    sizes = (
        ("diff_q", DIFF_HEADS * 2 * DIFF_DK), ("diff_k", DIFF_HEADS * 2 * DIFF_DK), ("diff_v", DIFF_HEADS * DIFF_DV),
        ("mla_cq", MLA_Q_RANK), ("mla_ckv", MLA_KV_RANK), ("mla_kr", MLA_ROPE),
        ("swa_q", SWA_HEADS * SWA_DH), ("swa_k", SWA_KV_HEADS * SWA_DH), ("swa_v", SWA_KV_HEADS * SWA_DH),
        ("ax_q", AX_HEADS * AX_DH), ("ax_k", AX_KV_HEADS * AX_DH), ("ax_v", AX_KV_HEADS * AX_DH),
        ("gates", N_BRANCH * d),
    )
    src, dst, out = 0, 0, {}
    for name, n in sizes:
        align = GATE_ALIGN if name == "gates" else LANE
        dst = -(-dst // align) * align
        out[name] = (src, dst, n)
        src += n
        dst += n
    return out, src, dst


def _pack_w_in(w_in_l, layout, n_packed_pad):
    d = w_in_l.shape[0]
    cols = []
    pos = 0
    for name, (src, dst, n) in layout.items():
        if dst > pos:
            cols.append(jnp.zeros((d, dst - pos), BF16))
        cols.append(w_in_l[:, src:src + n].astype(BF16))
        pos = dst + n
    if n_packed_pad > pos:
        cols.append(jnp.zeros((d, n_packed_pad - pos), BF16))
    return jnp.concatenate(cols, axis=1)


def kernel(x, c, ctx, c_ctx, ada_w, ada_b, norm1_g, norm2_g, w_in, diff_qn_g, diff_kn_g, diff_lambda, diff_subln_g, mla_qa_g, mla_w_uq, mla_kva_g, mla_w_ukv, mla_qn_g, mla_kn_g, swa_qn_g, swa_kn_g, swa_sink, ax_qn_g, ax_kn_g, w_branch, w_out, router_w, router_b, w_gu, b_gu, w_down, b_down):
    b, s, d = x.shape
    n_ctx = ctx.shape[1]
    depth = ada_w.shape[0]
    sall = s + n_ctx
    t_lat = b * s
    t_all = t_lat + b * n_ctx
    tm = ROW_TILE
    assert s % tm == 0 and (b * n_ctx) % tm == 0 and b + 1 <= 8
    tiles_per_batch = s // tm

    def grp_of_tile(i):
        return jnp.minimum(i // tiles_per_batch, b)

    grid_rows = s // GRID_W
    rows = jnp.repeat(jnp.arange(grid_rows, dtype=F32), GRID_W)
    cols = jnp.tile(jnp.arange(GRID_W, dtype=F32), grid_rows)
    cos64, sin64 = _rope_tables(rows, cols, DIFF_DK, n_ctx)
    cos128, sin128 = _rope_tables(rows, cols, SWA_DH, n_ctx)

    cond = jnp.concatenate([jax.nn.silu(c), jax.nn.silu(c_ctx)[None, :], jnp.zeros((8 - b - 1, d), F32)], axis=0)
    mod = ada_modulation(cond, ada_w, ada_b)[:, :b + 1]
    mod = mod.reshape(depth, b + 1, 6, d)

    layout, n_src, n_packed = _w_in_layout(d)
    tn_in = 1024
    n_packed_pad = -(-n_packed // tn_in) * tn_in

    def to_bs(a):
        cw = a.shape[-1]
        return jnp.concatenate([a[:t_lat].reshape(b, s, cw), a[t_lat:].reshape(b, n_ctx, cw)], axis=1)

    def to_rows(lat, cx):
        return jnp.concatenate([lat.reshape(t_lat, -1), cx.reshape(b * n_ctx, -1)], axis=0)

    xs = jnp.concatenate([x.reshape(t_lat, d), ctx.reshape(b * n_ctx, d)], axis=0)

    for l in range(depth):
        with_ctx = l < depth - 1
        lam_init = 0.8 - 0.6 * math.exp(-0.3 * l)
        sh1, sc1, g1, sh2, sc2, g2 = [mod[l, :, i] for i in range(6)]

        a1 = (norm1_g[l][None, :] * (1.0 + sc1))[:, None, :]
        p = norm_matmul(xs, 0, d, a1, sh1[:, None, :], grp_of_tile,
                        _pack_w_in(w_in[l], layout, n_packed_pad), tn_in, BF16)

        def seg(name):
            _, dst, n = layout[name]
            return p[:, dst:dst + n]

        scale = DIFF_DK ** -0.5 * LOG2E
        q = _rope(_rms(to_bs(seg("diff_q")).reshape(b, sall, 2 * DIFF_HEADS, DIFF_DK), diff_qn_g[l]), cos64, sin64) * scale
        k = _rope(_rms(to_bs(seg("diff_k")).reshape(b, sall, 2 * DIFF_HEADS, DIFF_DK), diff_kn_g[l]), cos64, sin64)
        q = q.reshape(b, sall, DIFF_HEADS, 2, DIFF_DK)
        zq = jnp.zeros_like(q[:, :, :, 0])
        qa = jnp.concatenate([q[:, :, :, 0], zq], axis=-1)
        qb = jnp.concatenate([zq, q[:, :, :, 1]], axis=-1)
        q = jnp.stack([qa, qb], axis=2).transpose(0, 3, 2, 1, 4).astype(BF16)
        k = k.reshape(b, sall, DIFF_HEADS, 2 * DIFF_DK).transpose(0, 2, 1, 3).astype(BF16)
        v = to_bs(seg("diff_v")).reshape(b, sall, DIFF_HEADS, DIFF_DV).transpose(0, 2, 3, 1)
        lf = diff_lambda[l].astype(F32)
        lam_val = (jnp.exp(jnp.sum(lf[0] * lf[1])) - jnp.exp(jnp.sum(lf[2] * lf[3])) + lam_init).reshape(1)
        sg = (diff_subln_g[l] * (1.0 - lam_init)).reshape(1, DIFF_DV)
        oa_l = flash_attention(q, k, v, sq=s, q_off_blk=0, skv=sall, kv_off_blk=0, tq=512, tk=_kv_chunk(sall), diff=(lam_val, sg))
        if with_ctx:
            oa_c = flash_attention(q, k, v, sq=n_ctx, q_off_blk=s // n_ctx, skv=n_ctx, kv_off_blk=s // n_ctx,
                                   tq=n_ctx, tk=n_ctx, diff=(lam_val, sg))

        scale = (MLA_NOPE + MLA_ROPE) ** -0.5 * LOG2E
        dkq = MLA_NOPE + MLA_ROPE
        cq_blk = layout["mla_cq"][1] // MLA_Q_RANK
        q_up = norm_matmul(p, cq_blk, MLA_Q_RANK, mla_qa_g[l].reshape(1, 1, -1), jnp.zeros((1, 1, MLA_Q_RANK), F32),
                           lambda i: 0, mla_w_uq[l].astype(BF16), MLA_HEADS * dkq, BF16)
        ckv_blk = layout["mla_ckv"][1] // MLA_KV_RANK
        kv_up = norm_matmul(p, ckv_blk, MLA_KV_RANK, mla_kva_g[l].reshape(1, 1, -1), jnp.zeros((1, 1, MLA_KV_RANK), F32),
                            lambda i: 0, mla_w_ukv[l].astype(BF16), 1024, BF16)
        q = _rms(to_bs(q_up).reshape(b, sall, MLA_HEADS, dkq), mla_qn_g[l])
        q = jnp.concatenate([q[..., :MLA_NOPE], _rope(q[..., MLA_NOPE:], cos64, sin64)], axis=-1) * scale
        q = jnp.pad(q, ((0, 0), (0, 0), (0, 0), (0, MLA_DKP - dkq)))
        q = q.transpose(0, 2, 1, 3)[:, :, None].astype(BF16)
        kv = to_bs(kv_up).reshape(b, sall, MLA_HEADS, MLA_NOPE + MLA_DV)
        kr = to_bs(seg("mla_kr"))
        k_rope = jnp.broadcast_to(kr[:, :, None, :], (b, sall, MLA_HEADS, MLA_ROPE))
        k = _rms(jnp.concatenate([kv[..., :MLA_NOPE], k_rope], axis=-1), mla_kn_g[l])
        k = jnp.concatenate([k[..., :MLA_NOPE], _rope(k[..., MLA_NOPE:], cos64, sin64)], axis=-1)
        k = jnp.pad(k, ((0, 0), (0, 0), (0, 0), (0, MLA_DKP - dkq))).transpose(0, 2, 1, 3).astype(BF16)
        v = kv[..., MLA_NOPE:].transpose(0, 2, 3, 1)
        ob_l = flash_attention(q, k, v, sq=s, q_off_blk=0, skv=sall, kv_off_blk=0, tq=1024, tk=_kv_chunk(sall))
        if with_ctx:
            ob_c = flash_attention(q, k, v, sq=n_ctx, q_off_blk=s // n_ctx, skv=n_ctx, kv_off_blk=s // n_ctx,
                                   tq=n_ctx, tk=n_ctx)

        def gqa(prefix, n_heads, n_kv, dh, qn, kn):
            rr = n_heads // n_kv
            qq = _rope(_rms(to_bs(seg(prefix + "_q")).reshape(b, sall, n_heads, dh), qn), cos128, sin128) * (dh ** -0.5 * LOG2E)
            kk = _rope(_rms(to_bs(seg(prefix + "_k")).reshape(b, sall, n_kv, dh), kn), cos128, sin128)
            vv = to_bs(seg(prefix + "_v")).reshape(b, sall, n_kv, dh)
            qq = qq.reshape(b, sall, n_kv, rr, dh).transpose(0, 2, 3, 1, 4).astype(BF16)
            return qq, kk.transpose(0, 2, 1, 3).astype(BF16), vv.transpose(0, 2, 3, 1)

        rr = SWA_HEADS // SWA_KV_HEADS
        q, k, v = gqa("swa", SWA_HEADS, SWA_KV_HEADS, SWA_DH, swa_qn_g[l], swa_kn_g[l])
        sink_gr = swa_sink[l].astype(F32).reshape(SWA_KV_HEADS, rr) * LOG2E

        def sink_chains(tq_):
            return jnp.repeat(sink_gr, tq_, axis=1).reshape(SWA_KV_HEADS, -1, 1, _chain_width(rr * tq_))

        tq_w = 256
        oc_l = window_attention(q, k, v, sink_chains(tq_w), s_lat=s, n_ctx=n_ctx, tq=tq_w)
        if with_ctx:
            oc_c = flash_attention(q, k, v, sq=n_ctx, q_off_blk=s // n_ctx, skv=n_ctx, kv_off_blk=s // n_ctx,
                                   tq=n_ctx, tk=n_ctx, sink=sink_chains(n_ctx))

        q, k, v = gqa("ax", AX_HEADS, AX_KV_HEADS, AX_DH, ax_qn_g[l], ax_kn_g[l])
        od_l = flash_attention(q, k, v, sq=s, q_off_blk=0, skv=sall, kv_off_blk=0, tq=256, tk=_kv_chunk(sall))
        if with_ctx:
            od_c = flash_attention(q, k, v, sq=n_ctx, q_off_blk=s // n_ctx, skv=n_ctx, kv_off_blk=s // n_ctx,
                                   tq=n_ctx, tk=n_ctx)

        if with_ctx:
            outs = [to_rows(ol, oc) for ol, oc in ((oa_l, oa_c), (ob_l, ob_c), (oc_l, oc_c), (od_l, od_c))]
            t_rows = t_all
        else:
            outs = [ol.reshape(t_lat, BRANCH_W) for ol in (oa_l, ob_l, oc_l, od_l)]
            t_rows = t_lat
            xs = xs[:t_lat]
        y = merge_branches(outs, p, layout["gates"][1], w_branch[l].astype(BF16), t_rows)
        xs = proj_residual(y, w_out[l].astype(BF16), xs, g1[:, None, :], grp_of_tile)

        n_tiles = t_rows // tm
        grp_rows = jnp.repeat(grp_of_tile(jnp.arange(n_tiles)), tm)
        h2 = _rms(xs, norm2_g[l]) * (1.0 + sc2)[grp_rows] + sh2[grp_rows]
        logits = jnp.dot(h2, router_w[l], precision=lax.Precision.HIGHEST) + router_b[l].astype(F32)
        top_val, top_idx = lax.top_k(logits, TOP_K)
        top_w = jax.nn.softmax(top_val, axis=-1)
        n_assign = t_rows * TOP_K
        bm = MOE_BLK
        flat_e = top_idx.reshape(-1).astype(jnp.int32)
        iota_n = jnp.arange(n_assign, dtype=jnp.int32)
        sorted_e, order = lax.sort((flat_e, iota_n), num_keys=1)
        counts = jnp.sum((flat_e[:, None] == jnp.arange(N_EXPERTS, dtype=jnp.int32)[None, :]).astype(jnp.int32), axis=0)
        padded = (counts + bm - 1) // bm * bm
        pad_end = jnp.cumsum(padded)
        pad_start = pad_end - padded
        start = jnp.cumsum(counts) - counts
        n_blocks = -(-n_assign // bm) + N_EXPERTS
        blk_expert = jnp.minimum(jnp.searchsorted(pad_end, jnp.arange(n_blocks) * bm, side='right'),
                                 N_EXPERTS - 1).astype(jnp.int32)
        n_used = (pad_end[-1] // bm).astype(jnp.int32).reshape(1)
        slot_e = jnp.repeat(blk_expert, bm)
        slot_j = jnp.arange(n_blocks * bm, dtype=jnp.int32) - pad_start[slot_e]
        slot_ok = slot_j < counts[slot_e]
        slot_a = order[jnp.clip(start[slot_e] + slot_j, 0, n_assign - 1)]
        slot_tok = jnp.where(slot_ok, slot_a // TOP_K, 0)
        slot_w = jnp.where(slot_ok, top_w.reshape(-1)[slot_a], 0.0)
        dest = (pad_start[sorted_e] + iota_n - start[sorted_e]).astype(jnp.int32)
        slot_of = lax.sort((order, dest), num_keys=1)[1].reshape(t_rows, TOP_K)
        x_sorted = h2.astype(BF16)[slot_tok]
        yb = moe_ffn_blocks(blk_expert, n_used, x_sorted, w_gu[l].astype(BF16), b_gu[l], w_down[l].astype(BF16),
                            b_down[l], slot_w[:, None])
        y_tok = jnp.sum(yb[slot_of].astype(F32), axis=1)
        xs = xs + g2[grp_rows] * y_tok

    return xs[:t_lat].reshape(b, s, d)
```
